```python
import jax, jax.numpy as jnp
from jax import lax
import numpy as np

D_MODEL = 1024
BATCH = 8
SEQ = 2048
DEPTH = 1
DEC_BATCH = 4
DEC_SEQ = 8192
PAST_LEN = 128

N_MEM = 256
EPS = 1e-6
D_RNN = D_MODEL
RNN_BLOCKS = 8
RNN_BW = D_RNN // RNN_BLOCKS
CONV_W = 4
CONV_LEFT = 2
LRU_C = 8.0
GLA_HEADS = 4
D_GLA_K = D_MODEL // 2
D_GLA_V = D_MODEL
GLA_DK = D_GLA_K // GLA_HEADS
GLA_DV = D_GLA_V // GLA_HEADS
GLA_RANK = 16
GLA_NORMALIZER = 16.0
GLA_CHUNK = 64
IN_SPLITS = (D_RNN, D_RNN, D_GLA_K, D_GLA_K, D_GLA_V, D_GLA_V, 2 * GLA_RANK, D_MODEL, D_MODEL)
D_IN = sum(IN_SPLITS)
XA_HEADS = 4
XA_HD = D_MODEL // XA_HEADS
N_EXPERTS = 32
TOP_K = 4
D_FF = D_MODEL
SWIGLU_ALPHA = 1.702
SWIGLU_LIMIT = 7.0
MOE_BLOCK = 128

kernel_name = "hybrid_rglru_gla_moe_encoder"


def rmsnorm(x, w):
    x32 = x.astype(jnp.float32)
    y = x32 * lax.rsqrt(jnp.mean(x32 * x32, axis=-1, keepdims=True) + EPS)
    return (y * w.astype(jnp.float32)).astype(x.dtype)


def centred_depthwise_conv(x, w, b):
    S = x.shape[1]
    xp = jnp.pad(x, ((0, 0), (CONV_LEFT, CONV_W - 1 - CONV_LEFT), (0, 0)))
    out = b
    for t in range(CONV_W):
        out = out + xp[:, t:t + S] * w[t]
    return out


def rglru_direction(xc, w_a, b_a, w_x, b_x, lam, reverse):
    Bsz, S, _ = xc.shape
    xb = xc.reshape(Bsz, S, RNN_BLOCKS, RNN_BW)
    r = jax.nn.sigmoid((jnp.einsum('bshi,hij->bshj', xb, w_a).reshape(Bsz, S, D_RNN) + b_a).astype(jnp.float32))
    i = jax.nn.sigmoid((jnp.einsum('bshi,hij->bshj', xb, w_x).reshape(Bsz, S, D_RNN) + b_x).astype(jnp.float32))
    log_a = -LRU_C * r * jax.nn.softplus(-lam.astype(jnp.float32))
    a = jnp.exp(log_a)
    u = jnp.sqrt(-jnp.expm1(2.0 * log_a)) * (i * xc.astype(jnp.float32))

    def combine(c1, c2):
        a1, b1 = c1
        a2, b2 = c2
        return a1 * a2, a2 * b1 + b2

    _, h = lax.associative_scan(combine, (a, u), reverse=reverse, axis=1)
    return h


def gla_chunked(q, k, v, log_a, include_diag):
    Bsz, H, S, dk = q.shape
    dv = v.shape[-1]
    nc = S // GLA_CHUNK
    q = q.reshape(Bsz, H, nc, GLA_CHUNK, dk)
    k = k.reshape(Bsz, H, nc, GLA_CHUNK, dk)
    v = v.reshape(Bsz, H, nc, GLA_CHUNK, dv)
    b = jnp.cumsum(log_a.reshape(Bsz, H, nc, GLA_CHUNK, dk), axis=3)
    b_last = b[:, :, :, -1:]
    q_dec = q * jnp.exp(b)
    k_inv = k * jnp.exp(-b)
    scores = jnp.einsum('bhnid,bhnjd->bhnij', q_dec, k_inv)
    mask = jnp.tril(jnp.ones((GLA_CHUNK, GLA_CHUNK), dtype=bool), k=0 if include_diag else -1)
    scores = jnp.where(mask, scores, 0.0)
    o_intra = jnp.einsum('bhnij,bhnjv->bhniv', scores, v)
    k_end = k * jnp.exp(b_last - b)
    ds = jnp.einsum('bhnjd,bhnjv->nbhdv', k_end, v)
    decay = jnp.moveaxis(jnp.exp(b_last[:, :, :, 0]), 2, 0)

    def step(state, inp):
        dec, d = inp
        return dec[..., None] * state + d, state

    _, s_prev = lax.scan(step, jnp.zeros((Bsz, H, dk, dv), jnp.float32), (decay, ds))
    o_inter = jnp.einsum('bhnid,nbhdv->bhniv', q_dec, s_prev)
    return (o_intra + o_inter).reshape(Bsz, H, S, dv)


def gla_branch(q, k, v, g_out, lowrank, w_gk, b_gk, w_gn):
    Bsz, S, _ = q.shape

    def heads(t, d):
        return jnp.moveaxis(t.astype(jnp.float32).reshape(Bsz, S, GLA_HEADS, d), 2, 1)

    qh = heads(q, GLA_DK) * (GLA_DK ** -0.5)
    kh = heads(k, GLA_DK)
    vh = heads(v, GLA_DV)
    lr = lowrank.astype(jnp.float32).reshape(Bsz, S, 2, GLA_RANK)
    gk = jax.nn.log_sigmoid(jnp.einsum('bsdr,drk->dbsk', lr, w_gk.astype(jnp.float32))
                            + b_gk.astype(jnp.float32)[:, None, None, :]) / GLA_NORMALIZER
    gk_f = heads(gk[0], GLA_DK)
    gk_b = heads(gk[1], GLA_DK)
    o_f = gla_chunked(qh, kh, vh, gk_f, True)
    o_b = jnp.flip(gla_chunked(jnp.flip(qh, 2), jnp.flip(kh, 2), jnp.flip(vh, 2), jnp.flip(gk_b, 2), False), 2)
    o = o_f + o_b
    o = o * lax.rsqrt(jnp.mean(o * o, axis=-1, keepdims=True) + EPS) * w_gn.astype(jnp.float32)
    o = jnp.moveaxis(o, 1, 2).reshape(Bsz, S, D_GLA_V)
    return (o * jax.nn.silu(g_out.astype(jnp.float32))).astype(q.dtype)


def mixer(h, w_in, conv_w, conv_b, rg_w_a, rg_b_a, rg_w_x, rg_b_x, rg_lambda,
          gla_w_gk, gla_b_gk, gla_norm, w_branch_rnn, w_branch_gla, w_out):
    offsets = [int(o) for o in np.cumsum(IN_SPLITS)[:-1]]
    x_rnn, g_rnn, q, k, v, g_gla, lowrank, m_rnn, m_gla = jnp.split(h @ w_in, offsets, axis=-1)
    xc = centred_depthwise_conv(x_rnn, conv_w, conv_b)
    h_f = rglru_direction(xc, rg_w_a[0], rg_b_a[0], rg_w_x[0], rg_b_x[0], rg_lambda[0], False)
    h_b = rglru_direction(xc, rg_w_a[1], rg_b_a[1], rg_w_x[1], rg_b_x[1], rg_lambda[1], True)
    y_rnn = ((h_f + h_b) * jax.nn.gelu(g_rnn.astype(jnp.float32))).astype(h.dtype)
    y_gla = gla_branch(q, k, v, g_gla, lowrank, gla_w_gk, gla_b_gk, gla_norm)
    z = jax.nn.sigmoid(m_rnn) * (y_rnn @ w_branch_rnn) + jax.nn.sigmoid(m_gla) * (y_gla @ w_branch_gla)
    return (z @ w_out).astype(h.dtype)


def cross_attention(h, mem_n, w_xq, w_xkv, w_xo):
    Bsz, S, _ = h.shape
    n_mem = mem_n.shape[1]
    q = (h @ w_xq).reshape(Bsz, S, XA_HEADS, XA_HD)
    kv = (mem_n @ w_xkv).reshape(Bsz, n_mem, 2, XA_HEADS, XA_HD)
    k, v = kv[:, :, 0], kv[:, :, 1]
    s = jnp.einsum('bqhd,bkhd->bhqk', q, k).astype(jnp.float32) * (XA_HD ** -0.5)
    p = jax.nn.softmax(s, axis=-1).astype(v.dtype)
    o = jnp.einsum('bhqk,bkhd->bqhd', p, v).reshape(Bsz, S, D_MODEL)
    return o @ w_xo


def moe(h, w_router, b_router, w_gate_up, b_gate_up, w_down, b_down):
    Bsz, S, D = h.shape
    T = Bsz * S
    xt = h.reshape(T, D)
    logits = (xt @ w_router + b_router).astype(jnp.float32)
    top_val, top_idx = lax.top_k(logits, TOP_K)
    gate = jax.nn.softmax(top_val, axis=-1)
    M = T * TOP_K
    e_flat = top_idx.reshape(M)
    tok_flat = jnp.arange(M, dtype=jnp.int32) // TOP_K
    g_flat = gate.reshape(M)
    order = jnp.argsort(e_flat)
    e_sorted, tok_sorted, g_sorted = e_flat[order], tok_flat[order], g_flat[order]
    counts = jnp.bincount(e_flat, length=N_EXPERTS)
    padded = (counts + MOE_BLOCK - 1) // MOE_BLOCK * MOE_BLOCK
    pend = jnp.cumsum(padded)
    pstart = pend - padded
    ustart = jnp.cumsum(counts) - counts
    pos = pstart[e_sorted] + (jnp.arange(M, dtype=jnp.int32) - ustart[e_sorted])
    nb = -(-M // MOE_BLOCK) + N_EXPERTS
    P = nb * MOE_BLOCK
    tok_pad = jnp.full((P,), T, dtype=jnp.int32).at[pos].set(tok_sorted)
    g_pad = jnp.zeros((P,), jnp.float32).at[pos].set(g_sorted)
    blk_start = jnp.arange(nb, dtype=jnp.int32) * MOE_BLOCK
    blk_expert = jnp.minimum(jnp.sum(pend[None, :] <= blk_start[:, None], axis=1), N_EXPERTS - 1)
    x_ext = jnp.concatenate([xt, jnp.zeros((1, D), xt.dtype)], axis=0)
    x_blk = x_ext[tok_pad].reshape(nb, MOE_BLOCK, D)

    def expert_block(args):
        xb, e = args
        gu = xb @ w_gate_up[e] + b_gate_up[e]
        x_glu = jnp.minimum(gu[:, ::2], SWIGLU_LIMIT)
        x_lin = jnp.clip(gu[:, 1::2], -SWIGLU_LIMIT, SWIGLU_LIMIT)
        act = x_glu * jax.nn.sigmoid(SWIGLU_ALPHA * x_glu) * (x_lin + 1.0)
        return act @ w_down[e] + b_down[e]

    y_blk = lax.map(expert_block, (x_blk, blk_expert))
    y = y_blk.reshape(P, D) * g_pad[:, None]
    out = jax.ops.segment_sum(y, tok_pad, num_segments=T + 1)[:T]
    return out.astype(h.dtype).reshape(Bsz, S, D)


def encoder(x, mem, layer_params, norm_final):
    for l in range(DEPTH):
        (norm_mix, w_in, conv_w, conv_b, rg_w_a, rg_b_a, rg_w_x, rg_b_x, rg_lambda,
         gla_w_gk, gla_b_gk, gla_norm, w_branch_rnn, w_branch_gla, w_out,
         norm_xattn, norm_mem, w_xq, w_xkv, w_xo,
         norm_ffn, w_router, b_router, w_gate_up, b_gate_up, w_down, b_down) = [w[l] for w in layer_params]
        x = x + mixer(rmsnorm(x, norm_mix), w_in, conv_w, conv_b, rg_w_a, rg_b_a, rg_w_x, rg_b_x, rg_lambda,
                      gla_w_gk, gla_b_gk, gla_norm, w_branch_rnn, w_branch_gla, w_out)
        x = x + cross_attention(rmsnorm(x, norm_xattn), rmsnorm(mem, norm_mem), w_xq, w_xkv, w_xo)
        x = x + moe(rmsnorm(x, norm_ffn), w_router, b_router, w_gate_up, b_gate_up, w_down, b_down)
    return rmsnorm(x, norm_final)


def setup_inputs(seed: int = 0) -> dict:
    key = jax.random.key(seed)
    ks = iter(jax.random.split(key, 48))
    L = DEPTH

    def nrm(shape, scale):
        return jax.random.normal(next(ks), shape, jnp.float32) * scale

    def gain(shape):
        return 1.0 + nrm(shape, 0.02)

    u = jax.random.uniform(next(ks), (L, 2, D_RNN), jnp.float32, 0.9, 0.999)
    s = u ** (1.0 / LRU_C)
    rg_lambda = jnp.log(s) - jnp.log1p(-s)
    return {
        "x_prompt": nrm((BATCH, SEQ, D_MODEL), 1.0),
        "x_sample": nrm((DEC_BATCH, DEC_SEQ, D_MODEL), 1.0),
        "mem_prompt": nrm((BATCH, N_MEM, D_MODEL), 1.0),
        "mem_sample": nrm((DEC_BATCH, N_MEM, D_MODEL), 1.0),
        "norm_mix": gain((L, D_MODEL)),
        "w_in": nrm((L, D_MODEL, D_IN), D_MODEL ** -0.5),
        "conv_w": nrm((L, CONV_W, D_RNN), CONV_W ** -0.5),
        "conv_b": nrm((L, D_RNN), 0.02),
        "rg_w_a": nrm((L, 2, RNN_BLOCKS, RNN_BW, RNN_BW), RNN_BW ** -0.5),
        "rg_b_a": nrm((L, 2, D_RNN), 0.02),
        "rg_w_x": nrm((L, 2, RNN_BLOCKS, RNN_BW, RNN_BW), RNN_BW ** -0.5),
        "rg_b_x": nrm((L, 2, D_RNN), 0.02),
        "rg_lambda": rg_lambda,
        "gla_w_gk": nrm((L, 2, GLA_RANK, D_GLA_K), GLA_RANK ** -0.5),
        "gla_b_gk": nrm((L, 2, D_GLA_K), 0.02),
        "gla_norm": gain((L, GLA_DV)),
        "w_branch_rnn": nrm((L, D_RNN, D_MODEL), D_RNN ** -0.5),
        "w_branch_gla": nrm((L, D_GLA_V, D_MODEL), D_GLA_V ** -0.5),
        "w_out": nrm((L, D_MODEL, D_MODEL), D_MODEL ** -0.5),
        "norm_xattn": gain((L, D_MODEL)),
        "norm_mem": gain((L, D_MODEL)),
        "w_xq": nrm((L, D_MODEL, D_MODEL), D_MODEL ** -0.5),
        "w_xkv": nrm((L, D_MODEL, 2 * D_MODEL), D_MODEL ** -0.5),
        "w_xo": nrm((L, D_MODEL, D_MODEL), D_MODEL ** -0.5),
        "norm_ffn": gain((L, D_MODEL)),
        "w_router": nrm((L, D_MODEL, N_EXPERTS), D_MODEL ** -0.5),
        "b_router": nrm((L, N_EXPERTS), 0.01),
        "w_gate_up": nrm((L, N_EXPERTS, D_MODEL, 2 * D_FF), D_MODEL ** -0.5),
        "b_gate_up": nrm((L, N_EXPERTS, 2 * D_FF), 0.02),
        "w_down": nrm((L, N_EXPERTS, D_FF, D_MODEL), D_FF ** -0.5),
        "b_down": nrm((L, N_EXPERTS, D_MODEL), 0.02),
        "norm_final": gain((D_MODEL,)),
    }


def reference(x_prompt, x_sample, mem_prompt, mem_sample, norm_mix, w_in, conv_w, conv_b,
              rg_w_a, rg_b_a, rg_w_x, rg_b_x, rg_lambda, gla_w_gk, gla_b_gk, gla_norm,
              w_branch_rnn, w_branch_gla, w_out, norm_xattn, norm_mem, w_xq, w_xkv, w_xo,
              norm_ffn, w_router, b_router, w_gate_up, b_gate_up, w_down, b_down, norm_final):
    layer_params = (norm_mix, w_in, conv_w, conv_b, rg_w_a, rg_b_a, rg_w_x, rg_b_x, rg_lambda,
                    gla_w_gk, gla_b_gk, gla_norm, w_branch_rnn, w_branch_gla, w_out,
                    norm_xattn, norm_mem, w_xq, w_xkv, w_xo,
                    norm_ffn, w_router, b_router, w_gate_up, b_gate_up, w_down, b_down)
    y_prompt = encoder(x_prompt, mem_prompt, layer_params, norm_final)
    y_sample = encoder(x_sample, mem_sample, layer_params, norm_final)
    return (y_prompt, y_sample)
```

```python
import functools

import jax
import jax.numpy as jnp
from jax import lax
from jax.experimental import pallas as pl
from jax.experimental.pallas import tpu as pltpu

F32 = jnp.float32
BF16 = jnp.bfloat16
I32 = jnp.int32

D_MODEL = 1024
EPS = 1e-6
LANES = 128
SUBLANES = 8

RNN_BLOCKS = 8
RNN_BW = D_MODEL // RNN_BLOCKS
CONV_W = 4
CONV_LEFT = 2
LRU_C = 8.0
RNN_CHUNK = 256
GLA_HEADS = 4
GLA_DK = 128
GLA_DV = 256
GLA_RANK = 16
GLA_NORMALIZER = 16.0
GLA_CHUNK = 64
GLA_SEG = 1024
XA_HEADS = 4
XA_HD = D_MODEL // XA_HEADS
N_EXPERTS = 32
TOP_K = 4
SWIGLU_ALPHA = 1.702
SWIGLU_LIMIT = 7.0
MOE_TM = 256

C_XRNN, C_GRNN, C_Q, C_K, C_V, C_GGLA, C_MRNN, C_MGLA, C_LR = 0, 1024, 2048, 2560, 3072, 4096, 5120, 6144, 7168
IN_COLS = 7296
IN_TN = 2432

VMEM_LIMIT = 48 * 1024 * 1024


def _cparams(sem):
    return pltpu.CompilerParams(dimension_semantics=sem, vmem_limit_bytes=VMEM_LIMIT)


def _rms(x, w):
    return x * lax.rsqrt(jnp.mean(x * x, axis=-1, keepdims=True) + EPS) * w


def _sigmoid(x):
    return 1.0 / (1.0 + jnp.exp(-x))


def _softplus(x):
    return jnp.maximum(x, 0.0) + jnp.log1p(jnp.exp(-jnp.abs(x)))


def _norm_matmul_kernel(x_ref, nw_ref, w_ref, o_ref):
    h = _rms(x_ref[...], nw_ref[...]).astype(BF16)
    o_ref[...] = jnp.dot(h, w_ref[...], preferred_element_type=F32).astype(o_ref.dtype)


def _norm_matmul(x, nw, w, tm, tn):
    T, D = x.shape
    N = w.shape[1]
    return pl.pallas_call(
        _norm_matmul_kernel,
        grid=(N // tn, T // tm),
        in_specs=[
            pl.BlockSpec((tm, D), lambda j, i: (i, 0)),
            pl.BlockSpec((1, D), lambda j, i: (0, 0)),
            pl.BlockSpec((D, tn), lambda j, i: (0, j)),
        ],
        out_specs=pl.BlockSpec((tm, tn), lambda j, i: (i, j)),
        out_shape=jax.ShapeDtypeStruct((T, N), BF16),
        compiler_params=_cparams(("parallel", "parallel")),
        name="norm_matmul",
    )(x, nw, w)


def _scan_chunk(a, u, carry, reverse, write_tile):
    R = a.shape[0]
    n = R // SUBLANES
    a3 = a.reshape(n, SUBLANES, LANES)
    h3 = u.reshape(n, SUBLANES, LANES)
    row = lax.broadcasted_iota(I32, (n, SUBLANES, LANES), 1)
    for s in (1, 2, 4):
        if reverse:
            keep = row < SUBLANES - s
            shift = SUBLANES - s
        else:
            keep = row >= s
            shift = s
        a_sh = jnp.where(keep, pltpu.roll(a3, shift, axis=1), 1.0)
        h_sh = jnp.where(keep, pltpu.roll(h3, shift, axis=1), 0.0)
        h3 = a3 * h_sh + h3
        a3 = a3 * a_sh
    order = range(n - 1, -1, -1) if reverse else range(n)
    for i in order:
        hi = h3[i] + a3[i] * carry
        write_tile(i, hi)
        last = hi[0:1, :] if reverse else hi[SUBLANES - 1:SUBLANES, :]
        carry = jnp.broadcast_to(last, (SUBLANES, LANES))
    return carry


def _rglru_kernel(x_ref, g_ref, p_ref, wg_ref, bg_ref, o_ref, xs_ref, hf_ref, *, S, R):
    nchunk = S // R
    pad = SUBLANES
    zeros = jnp.zeros((pad, LANES), F32)
    xs_ref[0:pad, :] = zeros
    xs_ref[pad + S:2 * pad + S, :] = zeros
    xs_ref[pad:pad + S, :] = x_ref[...].astype(F32)

    p = p_ref[...]
    taps = [p[t:t + 1, :] for t in range(CONV_W)]
    conv_b = p[CONV_W:CONV_W + 1, :]
    sp_f = _softplus(-p[5:6, :])
    sp_b = _softplus(-p[6:7, :])
    wg = wg_ref[...]
    bg = bg_ref[...]

    def conv_chunk(r0):
        xw = xs_ref[pl.ds(r0, R + 2 * pad), :]
        acc = conv_b
        for t in range(CONV_W):
            sh = (CONV_LEFT - t) % (R + 2 * pad)
            xt = xw if sh == 0 else pltpu.roll(xw, sh, axis=0)
            acc = acc + xt[pad:pad + R, :] * taps[t]
        return acc

    def gates(xc, d, sp):
        z = jnp.dot(xc.astype(BF16), wg[:, 2 * RNN_BW * d:2 * RNN_BW * (d + 1)],
                    preferred_element_type=F32) + bg[:, 2 * RNN_BW * d:2 * RNN_BW * (d + 1)]
        r = _sigmoid(z[:, :RNN_BW])
        i = _sigmoid(z[:, RNN_BW:])
        a = jnp.exp((-LRU_C * r) * sp)
        u = jnp.sqrt(1.0 - a * a) * (i * xc)
        return a, u

    def fwd_body(c, carry):
        r0 = pl.multiple_of(c * R, R)
        xc = conv_chunk(r0)
        a, u = gates(xc, 0, sp_f)

        def write_tile(i, hi):
            hf_ref[pl.ds(r0 + i * SUBLANES, SUBLANES), :] = hi

        return _scan_chunk(a, u, carry, False, write_tile)

    lax.fori_loop(0, nchunk, fwd_body, jnp.zeros((SUBLANES, LANES), F32))

    def bwd_body(c, carry):
        r0 = pl.multiple_of((nchunk - 1 - c) * R, R)
        xc = conv_chunk(r0)
        a, u = gates(xc, 1, sp_b)

        def write_tile(i, hi):
            rows = pl.ds(r0 + i * SUBLANES, SUBLANES)
            hf_ref[rows, :] = hf_ref[rows, :] + hi

        carry = _scan_chunk(a, u, carry, True, write_tile)
        g = g_ref[pl.ds(r0, R), :].astype(F32)
        o_ref[pl.ds(r0, R), :] = (hf_ref[pl.ds(r0, R), :] * jax.nn.gelu(g)).astype(o_ref.dtype)
        return carry

    lax.fori_loop(0, nchunk, bwd_body, jnp.zeros((SUBLANES, LANES), F32))


def _rglru(proj3, p_rnn, wg_rnn, bg_rnn):
    B, S, _ = proj3.shape
    R = min(RNN_CHUNK, S)
    kern = functools.partial(_rglru_kernel, S=S, R=R)
    return pl.pallas_call(
        kern,
        grid=(B, RNN_BLOCKS),
        in_specs=[
            pl.BlockSpec((None, S, RNN_BW), lambda b, h: (b, 0, C_XRNN // RNN_BW + h)),
            pl.BlockSpec((None, S, RNN_BW), lambda b, h: (b, 0, C_GRNN // RNN_BW + h)),
            pl.BlockSpec((None, SUBLANES, RNN_BW), lambda b, h: (h, 0, 0)),
            pl.BlockSpec((None, RNN_BW, 4 * RNN_BW), lambda b, h: (h, 0, 0)),
            pl.BlockSpec((None, 1, 4 * RNN_BW), lambda b, h: (h, 0, 0)),
        ],
        out_specs=pl.BlockSpec((None, S, RNN_BW), lambda b, h: (b, 0, h)),
        out_shape=jax.ShapeDtypeStruct((B, S, D_MODEL), BF16),
        scratch_shapes=[pltpu.VMEM((S + 2 * SUBLANES, LANES), F32), pltpu.VMEM((S, LANES), F32)],
        compiler_params=_cparams(("parallel", "parallel")),
        name="rglru",
    )(proj3, proj3, p_rnn, wg_rnn, bg_rnn)


def _gla_pass(q_ref, k_ref, v_ref, lr_ref, wgk_ref, bgk_ref, st_ref, reverse, seg, emit):
    L = GLA_CHUNK
    z = jnp.dot(lr_ref[...], wgk_ref[...], preferred_element_type=F32) + bgk_ref[...]
    gk_all = (jnp.minimum(z, 0.0) - jnp.log1p(jnp.exp(-jnp.abs(z)))) * (1.0 / GLA_NORMALIZER)
    ri = lax.broadcasted_iota(I32, (L, L), 0)
    ci = lax.broadcasted_iota(I32, (L, L), 1)
    if reverse:
        tri = (ci >= ri).astype(F32)
        smask = ci > ri
        edge = slice(0, 1)
    else:
        tri = (ci <= ri).astype(F32)
        smask = ci <= ri
        edge = slice(L - 1, L)
    st = st_ref[...]
    nck = seg // L
    order = range(nck - 1, -1, -1) if reverse else range(nck)
    scale = GLA_DK ** -0.5
    for c in order:
        rows = slice(c * L, (c + 1) * L)
        gk = gk_all[rows, :]
        b = jnp.dot(tri, gk, preferred_element_type=F32, precision=lax.Precision.HIGHEST)
        b_edge = b[edge, :]
        q = q_ref[rows, :].astype(F32) * scale
        k = k_ref[rows, :].astype(F32)
        v = v_ref[rows, :]
        qd = (q * jnp.exp(b)).astype(BF16)
        ki = (k * jnp.exp(-b)).astype(BF16)
        ke = (k * jnp.exp(b_edge - b)).astype(BF16)
        s = lax.dot_general(qd, ki, (((1,), (1,)), ((), ())), preferred_element_type=F32)
        s = jnp.where(smask, s, 0.0).astype(BF16)
        o = jnp.dot(s, v, preferred_element_type=F32)
        o = o + lax.dot_general(qd, st.astype(BF16), (((1,), (1,)), ((), ())), preferred_element_type=F32)
        emit(c * L, o)
        st = st * jnp.exp(b_edge) + lax.dot_general(v, ke, (((0,), (0,)), ((), ())), preferred_element_type=F32)
    st_ref[...] = st


def _gla_fwd_kernel(q_ref, k_ref, v_ref, lr_ref, wgk_ref, bgk_ref, o_ref, st_ref, *, seg):
    @pl.when(pl.program_id(2) == 0)
    def _():
        st_ref[...] = jnp.zeros_like(st_ref)

    def emit(c0, o):
        o_ref[c0:c0 + GLA_CHUNK, :] = o

    _gla_pass(q_ref, k_ref, v_ref, lr_ref, wgk_ref, bgk_ref, st_ref, False, seg, emit)


def _gla_bwd_kernel(q_ref, k_ref, v_ref, lr_ref, wgk_ref, bgk_ref, of_ref, g_ref, gn_ref, o_ref, st_ref, *, seg):
    @pl.when(pl.program_id(2) == 0)
    def _():
        st_ref[...] = jnp.zeros_like(st_ref)

    gn = gn_ref[...]

    def emit(c0, ob):
        rows = slice(c0, c0 + GLA_CHUNK)
        o = of_ref[rows, :] + ob
        o = _rms(o, gn)
        g = g_ref[rows, :].astype(F32)
        o_ref[rows, :] = (o * (g * _sigmoid(g))).astype(o_ref.dtype)

    _gla_pass(q_ref, k_ref, v_ref, lr_ref, wgk_ref, bgk_ref, st_ref, True, seg, emit)


def _gla(proj3, wgk_pad, bgk, gn):
    B, S, _ = proj3.shape
    seg = min(GLA_SEG, S)
    nseg = S // seg
    H = GLA_HEADS

    def specs(smap):
        return [
            pl.BlockSpec((None, seg, GLA_DK), lambda b, h, s: (b, smap(s), C_Q // GLA_DK + h)),
            pl.BlockSpec((None, seg, GLA_DK), lambda b, h, s: (b, smap(s), C_K // GLA_DK + h)),
            pl.BlockSpec((None, seg, GLA_DV), lambda b, h, s: (b, smap(s), C_V // GLA_DV + h)),
            pl.BlockSpec((None, seg, LANES), lambda b, h, s: (b, smap(s), C_LR // LANES)),
        ]

    def wspecs(d):
        return [
            pl.BlockSpec((None, None, LANES, GLA_DK), lambda b, h, s: (d, h, 0, 0)),
            pl.BlockSpec((None, None, 1, GLA_DK), lambda b, h, s: (d, h, 0, 0)),
        ]

    fwd_map = lambda s: s
    bwd_map = lambda s: nseg - 1 - s
    o_f = pl.pallas_call(
        functools.partial(_gla_fwd_kernel, seg=seg),
        grid=(B, H, nseg),
        in_specs=specs(fwd_map) + wspecs(0),
        out_specs=pl.BlockSpec((None, seg, GLA_DV), lambda b, h, s: (b, s, h)),
        out_shape=jax.ShapeDtypeStruct((B, S, H * GLA_DV), F32),
        scratch_shapes=[pltpu.VMEM((GLA_DV, GLA_DK), F32)],
        compiler_params=_cparams(("parallel", "parallel", "arbitrary")),
        name="gla_fwd",
    )(proj3, proj3, proj3, proj3, wgk_pad, bgk)
    y = pl.pallas_call(
        functools.partial(_gla_bwd_kernel, seg=seg),
        grid=(B, H, nseg),
        in_specs=specs(bwd_map) + wspecs(1) + [
            pl.BlockSpec((None, seg, GLA_DV), lambda b, h, s: (b, bwd_map(s), h)),
            pl.BlockSpec((None, seg, GLA_DV), lambda b, h, s: (b, bwd_map(s), C_GGLA // GLA_DV + h)),
            pl.BlockSpec((1, GLA_DV), lambda b, h, s: (0, 0)),
        ],
        out_specs=pl.BlockSpec((None, seg, GLA_DV), lambda b, h, s: (b, bwd_map(s), h)),
        out_shape=jax.ShapeDtypeStruct((B, S, H * GLA_DV), BF16),
        scratch_shapes=[pltpu.VMEM((GLA_DV, GLA_DK), F32)],
        compiler_params=_cparams(("parallel", "parallel", "arbitrary")),
        name="gla_bwd",
    )(proj3, proj3, proj3, proj3, wgk_pad, bgk, o_f, proj3, gn)
    return y


def _mix_kernel(x_ref, yr_ref, yg_ref, mr_ref, mg_ref, k_ref, v_ref, wbr_ref, wbg_ref, wout_ref,
                nxa_ref, wxq_ref, wxo_ref, nff_ref, wr_ref, br_ref,
                x2_ref, hn_ref, idx_ref, gate_ref):
    zr = jnp.dot(yr_ref[...], wbr_ref[...], preferred_element_type=F32)
    zg = jnp.dot(yg_ref[...], wbg_ref[...], preferred_element_type=F32)
    z = _sigmoid(mr_ref[...].astype(F32)) * zr + _sigmoid(mg_ref[...].astype(F32)) * zg
    x1 = x_ref[...] + jnp.dot(z.astype(BF16), wout_ref[...], preferred_element_type=F32)

    hn = _rms(x1, nxa_ref[...]).astype(BF16)
    q = jnp.dot(hn, wxq_ref[...], preferred_element_type=F32).astype(BF16)
    heads = []
    for h in range(XA_HEADS):
        cols = slice(h * XA_HD, (h + 1) * XA_HD)
        s = lax.dot_general(q[:, cols], k_ref[:, cols], (((1,), (1,)), ((), ())),
                            preferred_element_type=F32) * (XA_HD ** -0.5)
        s = s - jnp.max(s, axis=-1, keepdims=True)
        e = jnp.exp(s)
        p = (e / jnp.sum(e, axis=-1, keepdims=True)).astype(BF16)
        heads.append(jnp.dot(p, v_ref[:, cols], preferred_element_type=F32).astype(BF16))
    o = jnp.concatenate(heads, axis=-1)
    x2 = x1 + jnp.dot(o, wxo_ref[...], preferred_element_type=F32)
    x2_ref[...] = x2

    hn2 = _rms(x2, nff_ref[...])
    hn_ref[...] = hn2
    logits = jnp.dot(hn2, wr_ref[...], preferred_element_type=F32,
                     precision=lax.Precision.HIGHEST) + br_ref[...]
    tm = logits.shape[0]
    lane = lax.broadcasted_iota(I32, (tm, LANES), 1)
    lane_f = lane.astype(F32)
    neg = jnp.float32(-jnp.inf)
    vals = jnp.where(lane < N_EXPERTS, logits, neg)
    idx_out = jnp.zeros((tm, LANES), I32)
    top = []
    for kk in range(TOP_K):
        m = jnp.max(vals, axis=-1, keepdims=True)
        ix = jnp.min(jnp.where(vals == m, lane_f, float(LANES)), axis=-1, keepdims=True).astype(I32)
        idx_out = jnp.where(lane == kk, ix, idx_out)
        top.append(m)
        vals = jnp.where(lane == ix, neg, vals)
    es = [jnp.exp(t - top[0]) for t in top]
    den = es[0] + es[1] + es[2] + es[3]
    gate_out = jnp.zeros((tm, LANES), F32)
    for kk in range(TOP_K):
        gate_out = jnp.where(lane == kk, es[kk] / den, gate_out)
    idx_ref[...] = idx_out
    gate_ref[...] = gate_out


def _mix(x, y_rnn, y_gla, proj, kv, S, wbr, wbg, wout, nxa, wxq, wxo, nff, wr, br, tm):
    T, D = x.shape
    n_mem = kv.shape[1]
    per_b = S // tm
    full = lambda shape: pl.BlockSpec(shape, lambda i: (0,) * len(shape))
    row = lambda c: pl.BlockSpec((tm, D), lambda i: (i, c))
    return pl.pallas_call(
        _mix_kernel,
        grid=(T // tm,),
        in_specs=[
            row(0), row(0), row(0), row(C_MRNN // D), row(C_MGLA // D),
            pl.BlockSpec((None, n_mem, D), lambda i: (i // per_b, 0, 0)),
            pl.BlockSpec((None, n_mem, D), lambda i: (i // per_b, 0, 1)),
            full((D, D)), full((D, D)), full((D, D)),
            full((1, D)), full((D, D)), full((D, D)), full((1, D)),
            full((D, LANES)), full((1, LANES)),
        ],
        out_specs=[
            pl.BlockSpec((tm, D), lambda i: (i, 0)),
            pl.BlockSpec((tm, D), lambda i: (i, 0)),
            pl.BlockSpec((tm, LANES), lambda i: (i, 0)),
            pl.BlockSpec((tm, LANES), lambda i: (i, 0)),
        ],
        out_shape=[
            jax.ShapeDtypeStruct((T, D), F32),
            jax.ShapeDtypeStruct((T, D), F32),
            jax.ShapeDtypeStruct((T, LANES), I32),
            jax.ShapeDtypeStruct((T, LANES), F32),
        ],
        compiler_params=_cparams(("parallel",)),
        name="mix",
    )(x, y_rnn, y_gla, proj, proj, kv, kv, wbr, wbg, wout, nxa, wxq, wxo, nff, wr, br)


def _rank_kernel(idx_ref, rank_ref, cnt_ref, carry_ref):
    @pl.when(pl.program_id(0) == 0)
    def _():
        carry_ref[...] = jnp.zeros_like(carry_ref)

    idx = idx_ref[...]
    tm = idx.shape[0]
    lane = lax.broadcasted_iota(I32, (tm, LANES), 1)
    hot = [lane == idx[:, kk:kk + 1] for kk in range(TOP_K)]
    cnt = jnp.zeros((tm, LANES), F32)
    for kk in range(TOP_K):
        cnt = cnt + hot[kk].astype(F32)
    ri = lax.broadcasted_iota(I32, (tm, tm), 0)
    ci = lax.broadcasted_iota(I32, (tm, tm), 1)
    tri = (ci < ri).astype(BF16)
    cum = jnp.dot(tri, cnt.astype(BF16), preferred_element_type=F32) + carry_ref[0:1, :]
    out = jnp.zeros((tm, LANES), F32)
    for kk in range(TOP_K):
        rk = jnp.sum(jnp.where(hot[kk], cum, 0.0), axis=-1, keepdims=True)
        out = jnp.where(lane == kk, rk, out)
    rank_ref[...] = out.astype(I32)
    total = carry_ref[0:1, :] + jnp.sum(cnt, axis=0, keepdims=True)
    carry_ref[...] = jnp.broadcast_to(total, carry_ref.shape)
    cnt_ref[...] = jnp.broadcast_to(total, cnt_ref.shape).astype(I32)


def _rank(idx, tm):
    T = idx.shape[0]
    return pl.pallas_call(
        _rank_kernel,
        grid=(T // tm,),
        in_specs=[pl.BlockSpec((tm, LANES), lambda i: (i, 0))],
        out_specs=[pl.BlockSpec((tm, LANES), lambda i: (i, 0)),
                   pl.BlockSpec((SUBLANES, LANES), lambda i: (0, 0))],
        out_shape=[jax.ShapeDtypeStruct((T, LANES), I32),
                   jax.ShapeDtypeStruct((SUBLANES, LANES), I32)],
        scratch_shapes=[pltpu.VMEM((SUBLANES, LANES), F32)],
        compiler_params=_cparams(("arbitrary",)),
        name="moe_rank",
    )(idx)


def _pos_kernel(idx_ref, rank_ref, pstart_ref, pos_ref):
    idx = idx_ref[...]
    tm = idx.shape[0]
    lane = lax.broadcasted_iota(I32, (tm, LANES), 1)
    pstart = pstart_ref[0:1, :].astype(F32)
    out = rank_ref[...]
    for kk in range(TOP_K):
        base = jnp.sum(jnp.where(lane == idx[:, kk:kk + 1], pstart, 0.0), axis=-1, keepdims=True)
        out = jnp.where(lane == kk, out + base.astype(I32), out)
    pos_ref[...] = out


def _pos(idx, rank, pstart, tm):
    T = idx.shape[0]
    return pl.pallas_call(
        _pos_kernel,
        grid=(T // tm,),
        in_specs=[pl.BlockSpec((tm, LANES), lambda i: (i, 0)),
                  pl.BlockSpec((tm, LANES), lambda i: (i, 0)),
                  pl.BlockSpec((SUBLANES, LANES), lambda i: (0, 0))],
        out_specs=pl.BlockSpec((tm, LANES), lambda i: (i, 0)),
        out_shape=jax.ShapeDtypeStruct((T, LANES), I32),
        compiler_params=_cparams(("parallel",)),
        name="moe_pos",
    )(idx, rank, pstart)


def _dispatch_kernel(pos_ref, hn_ref, xin_ref, xblk_ref, sem):
    del xin_ref
    tm = hn_ref.shape[0]

    def copy(t, kk):
        p = pos_ref[t * TOP_K + kk]
        return pltpu.make_async_copy(hn_ref.at[pl.ds(t, 1), :], xblk_ref.at[pl.ds(p, 1), :], sem)

    def start(t, c):
        for kk in range(TOP_K):
            copy(t, kk).start()
        return c

    lax.fori_loop(0, tm, start, 0)

    def wait(t, c):
        for kk in range(TOP_K):
            copy(t, kk).wait()
        return c

    lax.fori_loop(0, tm, wait, 0)


def _dispatch(pos_flat, hn2, x_blk0, tm):
    T, D = hn2.shape
    return pl.pallas_call(
        _dispatch_kernel,
        grid=(T // tm,),
        in_specs=[pl.BlockSpec((tm * TOP_K,), lambda i: (i,), memory_space=pltpu.SMEM),
                  pl.BlockSpec((tm, D), lambda i: (i, 0)),
                  pl.BlockSpec(memory_space=pl.ANY)],
        out_specs=pl.BlockSpec(memory_space=pl.ANY),
        out_shape=jax.ShapeDtypeStruct(x_blk0.shape, x_blk0.dtype),
        scratch_shapes=[pltpu.SemaphoreType.DMA(())],
        input_output_aliases={2: 0},
        compiler_params=pltpu.CompilerParams(dimension_semantics=("arbitrary",), vmem_limit_bytes=VMEM_LIMIT,
                                             has_side_effects=True),
        name="moe_dispatch",
    )(pos_flat, hn2, x_blk0)


def _expert_kernel(be_ref, nu_ref, x_ref, wg_ref, wu_ref, bg_ref, bu_ref, wd_ref, bd_ref, y_ref):
    del be_ref
    i = pl.program_id(0)

    @pl.when(i < nu_ref[0])
    def _():
        xb = x_ref[...].astype(BF16)
        g = jnp.dot(xb, wg_ref[...], preferred_element_type=F32) + bg_ref[...]
        u = jnp.dot(xb, wu_ref[...], preferred_element_type=F32) + bu_ref[...]
        glu = jnp.minimum(g, SWIGLU_LIMIT)
        lin = jnp.clip(u, -SWIGLU_LIMIT, SWIGLU_LIMIT)
        act = glu * _sigmoid(SWIGLU_ALPHA * glu) * (lin + 1.0)
        y_ref[...] = jnp.dot(act.astype(BF16), wd_ref[...], preferred_element_type=F32) + bd_ref[...]

    @pl.when(i >= nu_ref[0])
    def _():
        y_ref[...] = jnp.zeros_like(y_ref)


def _experts(blk_expert, n_used, x_blk, wg, wu, bg, bu, wd, bd, tm):
    P, D = x_blk.shape
    F = wg.shape[2]
    nb = P // tm
    grid_spec = pltpu.PrefetchScalarGridSpec(
        num_scalar_prefetch=2,
        grid=(nb,),
        in_specs=[
            pl.BlockSpec((tm, D), lambda i, be, nu: (i, 0)),
            pl.BlockSpec((None, D, F), lambda i, be, nu: (be[i], 0, 0)),
            pl.BlockSpec((None, D, F), lambda i, be, nu: (be[i], 0, 0)),
            pl.BlockSpec((None, 1, F), lambda i, be, nu: (be[i], 0, 0)),
            pl.BlockSpec((None, 1, F), lambda i, be, nu: (be[i], 0, 0)),
            pl.BlockSpec((None, F, D), lambda i, be, nu: (be[i], 0, 0)),
            pl.BlockSpec((None, 1, D), lambda i, be, nu: (be[i], 0, 0)),
        ],
        out_specs=pl.BlockSpec((tm, D), lambda i, be, nu: (i, 0)),
    )
    return pl.pallas_call(
        _expert_kernel,
        grid_spec=grid_spec,
        out_shape=jax.ShapeDtypeStruct((P, D), F32),
        compiler_params=_cparams(("arbitrary",)),
        name="moe_experts",
    )(blk_expert, n_used, x_blk, wg, wu, bg, bu, wd, bd)


def _combine_kernel(pos_ref, gate_ref, x2_ref, nf_ref, y_ref, o_ref, ybuf, sem):
    tm = x2_ref.shape[0]

    def copy(t, kk):
        p = pos_ref[t * TOP_K + kk]
        return pltpu.make_async_copy(y_ref.at[pl.ds(p, 1), :], ybuf.at[kk, pl.ds(t, 1), :], sem)

    def start(t, c):
        for kk in range(TOP_K):
            copy(t, kk).start()
        return c

    lax.fori_loop(0, tm, start, 0)

    def wait(t, c):
        for kk in range(TOP_K):
            copy(t, kk).wait()
        return c

    lax.fori_loop(0, tm, wait, 0)

    gate = gate_ref[...]
    acc = jnp.zeros(x2_ref.shape, F32)
    for kk in range(TOP_K):
        acc = acc + ybuf[kk] * gate[:, kk:kk + 1]
    o_ref[...] = _rms(x2_ref[...] + acc, nf_ref[...])


def _combine(pos_flat, gate, x2, nf, y_blk, tm):
    T, D = x2.shape
    return pl.pallas_call(
        _combine_kernel,
        grid=(T // tm,),
        in_specs=[pl.BlockSpec((tm * TOP_K,), lambda i: (i,), memory_space=pltpu.SMEM),
                  pl.BlockSpec((tm, LANES), lambda i: (i, 0)),
                  pl.BlockSpec((tm, D), lambda i: (i, 0)),
                  pl.BlockSpec((1, D), lambda i: (0, 0)),
                  pl.BlockSpec(memory_space=pl.ANY)],
        out_specs=pl.BlockSpec((tm, D), lambda i: (i, 0)),
        out_shape=jax.ShapeDtypeStruct((T, D), F32),
        scratch_shapes=[pltpu.VMEM((TOP_K, tm, D), F32), pltpu.SemaphoreType.DMA(())],
        compiler_params=_cparams(("arbitrary",)),
        name="moe_combine",
    )(pos_flat, gate, x2, nf, y_blk)


def _prep_weights(norm_mix, w_in, conv_w, conv_b, rg_w_a, rg_b_a, rg_w_x, rg_b_x, rg_lambda,
                  gla_w_gk, gla_b_gk, gla_norm, w_branch_rnn, w_branch_gla, w_out,
                  norm_xattn, norm_mem, w_xq, w_xkv, w_xo, norm_ffn, w_router, b_router,
                  w_gate_up, b_gate_up, w_down, b_down, norm_final):
    l = 0
    D = D_MODEL
    w = w_in[l]
    w_main = w[:, :5120]
    w_lr = w[:, 5120:5152]
    w_m = w[:, 5152:7200]
    w_in_r = jnp.concatenate([w_main, w_m, w_lr, jnp.zeros((D, IN_COLS - 7200), F32)], axis=1).astype(BF16)

    p_rnn = jnp.concatenate([conv_w[l], conv_b[l][None], rg_lambda[l], jnp.zeros((1, D), F32)], axis=0)
    p_rnn = p_rnn.reshape(SUBLANES, RNN_BLOCKS, RNN_BW).transpose(1, 0, 2)
    wg_rnn = jnp.concatenate([rg_w_a[l, 0], rg_w_x[l, 0], rg_w_a[l, 1], rg_w_x[l, 1]], axis=2).astype(BF16)
    bg_rnn = jnp.stack([rg_b_a[l, 0], rg_b_x[l, 0], rg_b_a[l, 1], rg_b_x[l, 1]], axis=0)
    bg_rnn = bg_rnn.reshape(4, RNN_BLOCKS, RNN_BW).transpose(1, 0, 2).reshape(RNN_BLOCKS, 1, 4 * RNN_BW)

    wgk = gla_w_gk[l].reshape(2, GLA_RANK, GLA_HEADS, GLA_DK).transpose(0, 2, 1, 3)
    wgk_pad = jnp.zeros((2, GLA_HEADS, LANES, GLA_DK), F32)
    wgk_pad = wgk_pad.at[0, :, 0:GLA_RANK].set(wgk[0]).at[1, :, GLA_RANK:2 * GLA_RANK].set(wgk[1]).astype(BF16)
    bgk = gla_b_gk[l].reshape(2, GLA_HEADS, 1, GLA_DK)
    gn = gla_norm[l][None]

    wr = jnp.concatenate([w_router[l], jnp.zeros((D, LANES - N_EXPERTS), F32)], axis=1)
    br = jnp.concatenate([b_router[l], jnp.zeros((LANES - N_EXPERTS,), F32)])[None]

    wgu = w_gate_up[l]
    return dict(
        norm_mix=norm_mix[l][None], w_in=w_in_r, p_rnn=p_rnn, wg_rnn=wg_rnn, bg_rnn=bg_rnn,
        wgk=wgk_pad, bgk=bgk, gn=gn,
        wbr=w_branch_rnn[l].astype(BF16), wbg=w_branch_gla[l].astype(BF16), wout=w_out[l].astype(BF16),
        nxa=norm_xattn[l][None], nmem=norm_mem[l][None], wxq=w_xq[l].astype(BF16),
        wxkv=w_xkv[l].astype(BF16), wxo=w_xo[l].astype(BF16), nff=norm_ffn[l][None], wr=wr, br=br,
        wg=wgu[:, :, 0::2].astype(BF16), wu=wgu[:, :, 1::2].astype(BF16),
        bg=b_gate_up[l][:, None, 0::2], bu=b_gate_up[l][:, None, 1::2],
        wd=w_down[l].astype(BF16), bd=b_down[l][:, None, :],
        nf=norm_final[None],
    )


def _encoder(x, mem, W):
    B, S, D = x.shape
    T = B * S
    n_mem = mem.shape[1]
    xt = x.reshape(T, D)
    tm_in = min(512, T)
    proj = _norm_matmul(xt, W["norm_mix"], W["w_in"], tm_in, IN_TN)
    proj3 = proj.reshape(B, S, IN_COLS)
    y_rnn = _rglru(proj3, W["p_rnn"], W["wg_rnn"], W["bg_rnn"]).reshape(T, D)
    y_gla = _gla(proj3, W["wgk"], W["bgk"], W["gn"]).reshape(T, D)
    kv = _norm_matmul(mem.reshape(B * n_mem, D), W["nmem"], W["wxkv"], min(512, B * n_mem), 2 * D)
    kv = kv.reshape(B, n_mem, 2 * D)
    tm_mix = min(256, S)
    x2, hn2, idx, gate = _mix(xt, y_rnn, y_gla, proj, kv, S, W["wbr"], W["wbg"], W["wout"], W["nxa"],
                              W["wxq"], W["wxo"], W["nff"], W["wr"], W["br"], tm_mix)

    tm_r = min(512, T)
    rank, counts = _rank(idx, tm_r)
    cnt = counts[0, :N_EXPERTS]
    padded = (cnt + MOE_TM - 1) // MOE_TM * MOE_TM
    pend = jnp.cumsum(padded)
    pstart = pend - padded
    nb = (T * TOP_K) // MOE_TM + N_EXPERTS
    blk_start = jnp.arange(nb, dtype=I32) * MOE_TM
    blk_expert = jnp.minimum(jnp.sum(pend[None, :] <= blk_start[:, None], axis=1), N_EXPERTS - 1).astype(I32)
    n_used = (pend[-1] // MOE_TM).astype(I32)[None]
    pstart_row = jnp.zeros((SUBLANES, LANES), I32).at[:, :N_EXPERTS].set(pstart[None, :].astype(I32))
    pos = _pos(idx, rank, pstart_row, tm_r)
    pos_flat = pos[:, :TOP_K].reshape(T * TOP_K)

    x_blk0 = jnp.zeros((nb * MOE_TM, D), F32)
    x_blk = _dispatch(pos_flat, hn2, x_blk0, min(256, T))
    y_blk = _experts(blk_expert, n_used, x_blk, W["wg"], W["wu"], W["bg"], W["bu"], W["wd"], W["bd"], MOE_TM)
    out = _combine(pos_flat, gate, x2, W["nf"], y_blk, min(128, T))
    return out.reshape(B, S, D)


def kernel(x_prompt, x_sample, mem_prompt, mem_sample, norm_mix, w_in, conv_w, conv_b, rg_w_a, rg_b_a, rg_w_x, rg_b_x, rg_lambda, gla_w_gk, gla_b_gk, gla_norm, w_branch_rnn, w_branch_gla, w_out, norm_xattn, norm_mem, w_xq, w_xkv, w_xo, norm_ffn, w_router, b_router, w_gate_up, b_gate_up, w_down, b_down, norm_final):
    W = _prep_weights(norm_mix, w_in, conv_w, conv_b, rg_w_a, rg_b_a, rg_w_x, rg_b_x, rg_lambda,
                      gla_w_gk, gla_b_gk, gla_norm, w_branch_rnn, w_branch_gla, w_out,
                      norm_xattn, norm_mem, w_xq, w_xkv, w_xo, norm_ffn, w_router, b_router,
                      w_gate_up, b_gate_up, w_down, b_down, norm_final)
    y_prompt = _encoder(x_prompt, mem_prompt, W)
    y_sample = _encoder(x_sample, mem_sample, W)
    return (y_prompt, y_sample)
```

```python
import functools

import jax
import jax.numpy as jnp
from jax import lax
from jax.experimental import pallas as pl
from jax.experimental.pallas import tpu as pltpu

F32 = jnp.float32
BF16 = jnp.bfloat16
I32 = jnp.int32

D_MODEL = 1024
EPS = 1e-6
LANES = 128
SUBLANES = 8

RNN_BLOCKS = 8
RNN_BW = D_MODEL // RNN_BLOCKS
CONV_W = 4
CONV_LEFT = 2
LRU_C = 8.0
RNN_CHUNK = 256
GLA_HEADS = 4
GLA_DK = 128
GLA_DV = 256
GLA_RANK = 16
GLA_NORMALIZER = 16.0
GLA_CHUNK = 64
GLA_SEG = 1024
XA_HEADS = 4
XA_HD = D_MODEL // XA_HEADS
N_EXPERTS = 32
TOP_K = 4
SWIGLU_ALPHA = 1.702
SWIGLU_LIMIT = 7.0
MOE_TM = 512
MOE_TT = 256
SEG_ALIGN = SUBLANES
SEG_BITS = (256, 128, 64, 32, 16, 8)
PAD_BITS = (256, 128, 64, 32, 16, 8)
XS_ROWS = MOE_TT * TOP_K + N_EXPERTS * SEG_ALIGN

C_XRNN, C_GRNN, C_Q, C_K, C_V, C_GGLA, C_MRNN, C_MGLA, C_LR = 0, 1024, 2048, 2560, 3072, 4096, 5120, 6144, 7168
IN_COLS = 7296
IN_TN = 2432

VMEM_LIMIT = 48 * 1024 * 1024


def _cparams(sem):
    return pltpu.CompilerParams(dimension_semantics=sem, vmem_limit_bytes=VMEM_LIMIT)


def _rms(x, w):
    return x * lax.rsqrt(jnp.mean(x * x, axis=-1, keepdims=True) + EPS) * w


def _sigmoid(x):
    return 1.0 / (1.0 + jnp.exp(-x))


def _softplus(x):
    return jnp.maximum(x, 0.0) + jnp.log1p(jnp.exp(-jnp.abs(x)))


def _norm_matmul_kernel(x_ref, nw_ref, w_ref, o_ref):
    h = _rms(x_ref[...], nw_ref[...]).astype(BF16)
    o_ref[...] = jnp.dot(h, w_ref[...], preferred_element_type=F32).astype(o_ref.dtype)


def _norm_matmul(x, nw, w, tm, tn):
    T, D = x.shape
    N = w.shape[1]
    return pl.pallas_call(
        _norm_matmul_kernel,
        grid=(N // tn, T // tm),
        in_specs=[
            pl.BlockSpec((tm, D), lambda j, i: (i, 0)),
            pl.BlockSpec((1, D), lambda j, i: (0, 0)),
            pl.BlockSpec((D, tn), lambda j, i: (0, j)),
        ],
        out_specs=pl.BlockSpec((tm, tn), lambda j, i: (i, j)),
        out_shape=jax.ShapeDtypeStruct((T, N), BF16),
        compiler_params=_cparams(("parallel", "parallel")),
        name="norm_matmul",
    )(x, nw, w)


def _scan_chunk(a, u, carry, reverse, write_tile):
    R = a.shape[0]
    n = R // SUBLANES
    a3 = a.reshape(n, SUBLANES, LANES)
    h3 = u.reshape(n, SUBLANES, LANES)
    row = lax.broadcasted_iota(I32, (n, SUBLANES, LANES), 1)
    for s in (1, 2, 4):
        if reverse:
            keep = row < SUBLANES - s
            shift = SUBLANES - s
        else:
            keep = row >= s
            shift = s
        a_sh = jnp.where(keep, pltpu.roll(a3, shift, axis=1), 1.0)
        h_sh = jnp.where(keep, pltpu.roll(h3, shift, axis=1), 0.0)
        h3 = a3 * h_sh + h3
        a3 = a3 * a_sh
    order = range(n - 1, -1, -1) if reverse else range(n)
    for i in order:
        hi = h3[i] + a3[i] * carry
        write_tile(i, hi)
        last = hi[0:1, :] if reverse else hi[SUBLANES - 1:SUBLANES, :]
        carry = jnp.broadcast_to(last, (SUBLANES, LANES))
    return carry


def _rglru_kernel(x_ref, g_ref, p_ref, wg_ref, bg_ref, o_ref, xs_ref, hf_ref, ab_ref, ub_ref, *, S, R):
    nchunk = S // R
    pad = SUBLANES
    zeros = jnp.zeros((pad, LANES), F32)
    xs_ref[0:pad, :] = zeros
    xs_ref[pad + S:2 * pad + S, :] = zeros
    xs_ref[pad:pad + S, :] = x_ref[...].astype(F32)

    p = p_ref[...]
    taps = [p[t:t + 1, :] for t in range(CONV_W)]
    conv_b = p[CONV_W:CONV_W + 1, :]
    sp_f = _softplus(-p[5:6, :])
    sp_b = _softplus(-p[6:7, :])
    wg = wg_ref[...]
    bg = bg_ref[...]

    def conv_chunk(r0):
        xw = xs_ref[pl.ds(r0, R + 2 * pad), :]
        acc = conv_b
        for t in range(CONV_W):
            sh = (CONV_LEFT - t) % (R + 2 * pad)
            xt = xw if sh == 0 else pltpu.roll(xw, sh, axis=0)
            acc = acc + xt[pad:pad + R, :] * taps[t]
        return acc

    def decay_and_input(z, xc, d, sp):
        r = _sigmoid(z[:, 2 * RNN_BW * d:2 * RNN_BW * d + RNN_BW])
        i = _sigmoid(z[:, 2 * RNN_BW * d + RNN_BW:2 * RNN_BW * (d + 1)])
        a = jnp.exp((-LRU_C * r) * sp)
        u = jnp.sqrt(1.0 - a * a) * (i * xc)
        return a, u

    def fwd_body(c, carry):
        r0 = pl.multiple_of(c * R, R)
        xc = conv_chunk(r0)
        z = jnp.dot(xc.astype(BF16), wg, preferred_element_type=F32) + bg
        a_b, u_b = decay_and_input(z, xc, 1, sp_b)
        ab_ref[pl.ds(r0, R), :] = a_b
        ub_ref[pl.ds(r0, R), :] = u_b
        a, u = decay_and_input(z, xc, 0, sp_f)

        def write_tile(i, hi):
            hf_ref[pl.ds(r0 + i * SUBLANES, SUBLANES), :] = hi

        return _scan_chunk(a, u, carry, False, write_tile)

    lax.fori_loop(0, nchunk, fwd_body, jnp.zeros((SUBLANES, LANES), F32))

    def bwd_body(c, carry):
        r0 = pl.multiple_of((nchunk - 1 - c) * R, R)
        a = ab_ref[pl.ds(r0, R), :]
        u = ub_ref[pl.ds(r0, R), :]

        def write_tile(i, hi):
            rows = pl.ds(r0 + i * SUBLANES, SUBLANES)
            hf_ref[rows, :] = hf_ref[rows, :] + hi

        carry = _scan_chunk(a, u, carry, True, write_tile)
        g = g_ref[pl.ds(r0, R), :].astype(F32)
        o_ref[pl.ds(r0, R), :] = (hf_ref[pl.ds(r0, R), :] * jax.nn.gelu(g)).astype(o_ref.dtype)
        return carry

    lax.fori_loop(0, nchunk, bwd_body, jnp.zeros((SUBLANES, LANES), F32))


def _rglru(proj3, p_rnn, wg_rnn, bg_rnn):
    B, S, _ = proj3.shape
    R = min(RNN_CHUNK, S)
    kern = functools.partial(_rglru_kernel, S=S, R=R)
    return pl.pallas_call(
        kern,
        grid=(B, RNN_BLOCKS),
        in_specs=[
            pl.BlockSpec((None, S, RNN_BW), lambda b, h: (b, 0, C_XRNN // RNN_BW + h)),
            pl.BlockSpec((None, S, RNN_BW), lambda b, h: (b, 0, C_GRNN // RNN_BW + h)),
            pl.BlockSpec((None, SUBLANES, RNN_BW), lambda b, h: (h, 0, 0)),
            pl.BlockSpec((None, RNN_BW, 4 * RNN_BW), lambda b, h: (h, 0, 0)),
            pl.BlockSpec((None, 1, 4 * RNN_BW), lambda b, h: (h, 0, 0)),
        ],
        out_specs=pl.BlockSpec((None, S, RNN_BW), lambda b, h: (b, 0, h)),
        out_shape=jax.ShapeDtypeStruct((B, S, D_MODEL), BF16),
        scratch_shapes=[pltpu.VMEM((S + 2 * SUBLANES, LANES), F32), pltpu.VMEM((S, LANES), F32),
                        pltpu.VMEM((S, LANES), F32), pltpu.VMEM((S, LANES), F32)],
        compiler_params=_cparams(("parallel", "parallel")),
        name="rglru",
    )(proj3, proj3, p_rnn, wg_rnn, bg_rnn)


def _gla_pass(q_ref, k_ref, v_ref, lr_ref, wgk_ref, bgk_ref, st_ref, reverse, seg, emit):
    L = GLA_CHUNK
    z = jnp.dot(lr_ref[...], wgk_ref[...], preferred_element_type=F32) + bgk_ref[...]
    gk_all = (jnp.minimum(z, 0.0) - jnp.log1p(jnp.exp(-jnp.abs(z)))) * (1.0 / GLA_NORMALIZER)
    ri = lax.broadcasted_iota(I32, (L, L), 0)
    ci = lax.broadcasted_iota(I32, (L, L), 1)
    if reverse:
        tri = (ci >= ri).astype(F32)
        smask = ci > ri
        edge = slice(0, 1)
    else:
        tri = (ci <= ri).astype(F32)
        smask = ci <= ri
        edge = slice(L - 1, L)
    nck = seg // L
    order = range(nck - 1, -1, -1) if reverse else range(nck)
    scale = GLA_DK ** -0.5
    nt = (((1,), (1,)), ((), ()))
    tn = (((0,), (0,)), ((), ()))

    gk_wide = jnp.concatenate([gk_all[c * L:(c + 1) * L, :] for c in range(nck)], axis=1)
    b_wide = jnp.dot(tri, gk_wide, preferred_element_type=F32, precision=lax.Precision.HIGHEST)
    qd, ki, ke, dec = [], [], [], []
    for c in range(nck):
        rows = slice(c * L, (c + 1) * L)
        b = b_wide[:, c * GLA_DK:(c + 1) * GLA_DK]
        d_edge = jnp.exp(b[edge, :])
        q = q_ref[rows, :].astype(F32) * scale
        k_inv = k_ref[rows, :].astype(F32) * jnp.exp(-b)
        qd.append((q * jnp.exp(b)).astype(BF16))
        ki.append(k_inv.astype(BF16))
        ke.append((k_inv * d_edge).astype(BF16))
        dec.append(d_edge)
    sc = []
    for c in range(nck):
        s = lax.dot_general(qd[c], ki[c], nt, preferred_element_type=F32)
        sc.append(jnp.where(smask, s, 0.0).astype(BF16))
    o_intra, ds = [], []
    for c in range(nck):
        v = v_ref[c * L:(c + 1) * L, :]
        o_intra.append(jnp.dot(sc[c], v, preferred_element_type=F32))
        ds.append(lax.dot_general(v, ke[c], tn, preferred_element_type=F32))
    st = st_ref[...]
    for c in order:
        o = o_intra[c] + lax.dot_general(qd[c], st.astype(BF16), nt, preferred_element_type=F32)
        emit(c * L, o)
        st = st * dec[c] + ds[c]
    st_ref[...] = st


def _gla_fwd_kernel(q_ref, k_ref, v_ref, lr_ref, wgk_ref, bgk_ref, o_ref, st_ref, *, seg):
    @pl.when(pl.program_id(2) == 0)
    def _():
        st_ref[...] = jnp.zeros_like(st_ref)

    def emit(c0, o):
        o_ref[c0:c0 + GLA_CHUNK, :] = o

    _gla_pass(q_ref, k_ref, v_ref, lr_ref, wgk_ref, bgk_ref, st_ref, False, seg, emit)


def _gla_bwd_kernel(q_ref, k_ref, v_ref, lr_ref, wgk_ref, bgk_ref, of_ref, g_ref, gn_ref, o_ref, st_ref, *, seg):
    @pl.when(pl.program_id(2) == 0)
    def _():
        st_ref[...] = jnp.zeros_like(st_ref)

    gn = gn_ref[...]

    def emit(c0, ob):
        rows = slice(c0, c0 + GLA_CHUNK)
        o = of_ref[rows, :] + ob
        o = _rms(o, gn)
        g = g_ref[rows, :].astype(F32)
        o_ref[rows, :] = (o * (g * _sigmoid(g))).astype(o_ref.dtype)

    _gla_pass(q_ref, k_ref, v_ref, lr_ref, wgk_ref, bgk_ref, st_ref, True, seg, emit)


def _gla(proj3, wgk_pad, bgk, gn):
    B, S, _ = proj3.shape
    seg = min(GLA_SEG, S)
    nseg = S // seg
    H = GLA_HEADS

    def specs(smap):
        return [
            pl.BlockSpec((None, seg, GLA_DK), lambda b, h, s: (b, smap(s), C_Q // GLA_DK + h)),
            pl.BlockSpec((None, seg, GLA_DK), lambda b, h, s: (b, smap(s), C_K // GLA_DK + h)),
            pl.BlockSpec((None, seg, GLA_DV), lambda b, h, s: (b, smap(s), C_V // GLA_DV + h)),
            pl.BlockSpec((None, seg, LANES), lambda b, h, s: (b, smap(s), C_LR // LANES)),
        ]

    def wspecs(d):
        return [
            pl.BlockSpec((None, None, LANES, GLA_DK), lambda b, h, s: (d, h, 0, 0)),
            pl.BlockSpec((None, None, 1, GLA_DK), lambda b, h, s: (d, h, 0, 0)),
        ]

    fwd_map = lambda s: s
    bwd_map = lambda s: nseg - 1 - s
    o_f = pl.pallas_call(
        functools.partial(_gla_fwd_kernel, seg=seg),
        grid=(B, H, nseg),
        in_specs=specs(fwd_map) + wspecs(0),
        out_specs=pl.BlockSpec((None, seg, GLA_DV), lambda b, h, s: (b, s, h)),
        out_shape=jax.ShapeDtypeStruct((B, S, H * GLA_DV), F32),
        scratch_shapes=[pltpu.VMEM((GLA_DV, GLA_DK), F32)],
        compiler_params=_cparams(("parallel", "parallel", "arbitrary")),
        name="gla_fwd",
    )(proj3, proj3, proj3, proj3, wgk_pad, bgk)
    y = pl.pallas_call(
        functools.partial(_gla_bwd_kernel, seg=seg),
        grid=(B, H, nseg),
        in_specs=specs(bwd_map) + wspecs(1) + [
            pl.BlockSpec((None, seg, GLA_DV), lambda b, h, s: (b, bwd_map(s), h)),
            pl.BlockSpec((None, seg, GLA_DV), lambda b, h, s: (b, bwd_map(s), C_GGLA // GLA_DV + h)),
            pl.BlockSpec((1, GLA_DV), lambda b, h, s: (0, 0)),
        ],
        out_specs=pl.BlockSpec((None, seg, GLA_DV), lambda b, h, s: (b, bwd_map(s), h)),
        out_shape=jax.ShapeDtypeStruct((B, S, H * GLA_DV), BF16),
        scratch_shapes=[pltpu.VMEM((GLA_DV, GLA_DK), F32)],
        compiler_params=_cparams(("parallel", "parallel", "arbitrary")),
        name="gla_bwd",
    )(proj3, proj3, proj3, proj3, wgk_pad, bgk, o_f, proj3, gn)
    return y


MIX_TILE_INPUTS = 7


def _mix_kernel(*refs, bounds):
    ng = len(bounds)
    shared = refs[MIX_TILE_INPUTS * ng:]
    i = pl.program_id(0)
    for g, (t0, t1) in enumerate(bounds):
        @pl.when(jnp.logical_and(i >= t0, i < t1))
        def _(g=g):
            _mix_body(*refs[MIX_TILE_INPUTS * g:MIX_TILE_INPUTS * (g + 1)], *shared)


def _mix_body(x_ref, yr_ref, yg_ref, mr_ref, mg_ref, kt_ref, v_ref, wbr_ref, wbg_ref, wout_ref,
              nxa_ref, wxq_ref, wxo_ref, nff_ref, wr_ref, br_ref, x2_ref, hn_ref, idx_ref, gate_ref):
    zr = jnp.dot(yr_ref[...], wbr_ref[...], preferred_element_type=F32)
    zg = jnp.dot(yg_ref[...], wbg_ref[...], preferred_element_type=F32)
    z = _sigmoid(mr_ref[...].astype(F32)) * zr + _sigmoid(mg_ref[...].astype(F32)) * zg
    x1 = x_ref[...] + jnp.dot(z.astype(BF16), wout_ref[...], preferred_element_type=F32)

    hn = _rms(x1, nxa_ref[...]).astype(BF16)
    q = jnp.dot(hn, wxq_ref[...], preferred_element_type=F32).astype(BF16)
    heads = []
    for h in range(XA_HEADS):
        cols = slice(h * XA_HD, (h + 1) * XA_HD)
        s = jnp.dot(q[:, cols], kt_ref[cols, :], preferred_element_type=F32) * (XA_HD ** -0.5)
        s = s - jnp.max(s, axis=-1, keepdims=True)
        e = jnp.exp(s)
        p = (e / jnp.sum(e, axis=-1, keepdims=True)).astype(BF16)
        heads.append(jnp.dot(p, v_ref[:, cols], preferred_element_type=F32).astype(BF16))
    o = jnp.concatenate(heads, axis=-1)
    x2 = x1 + jnp.dot(o, wxo_ref[...], preferred_element_type=F32)
    x2_ref[...] = x2

    hn2 = _rms(x2, nff_ref[...])
    hn_ref[...] = hn2.astype(hn_ref.dtype)
    wr = wr_ref[...]
    wr_hi = wr.astype(BF16)
    wr_lo = (wr - wr_hi.astype(F32)).astype(BF16)
    hn_hi = hn2.astype(BF16)
    hn_lo = (hn2 - hn_hi.astype(F32)).astype(BF16)
    logits = (jnp.dot(hn_hi, wr_hi, preferred_element_type=F32)
              + jnp.dot(hn_lo, wr_hi, preferred_element_type=F32)
              + jnp.dot(hn_hi, wr_lo, preferred_element_type=F32)) + br_ref[...]
    tm = logits.shape[0]
    lane = lax.broadcasted_iota(I32, (tm, LANES), 1)
    lane_f = lane.astype(F32)
    neg = jnp.float32(-jnp.inf)
    vals = jnp.where(lane < N_EXPERTS, logits, neg)
    idx_out = jnp.zeros((tm, LANES), I32)
    top = []
    for kk in range(TOP_K):
        m = jnp.max(vals, axis=-1, keepdims=True)
        ix = jnp.min(jnp.where(vals == m, lane_f, float(LANES)), axis=-1, keepdims=True).astype(I32)
        idx_out = jnp.where(lane == kk, ix, idx_out)
        top.append(m)
        vals = jnp.where(lane == ix, neg, vals)
    es = [jnp.exp(t - top[0]) for t in top]
    den = es[0] + es[1] + es[2] + es[3]
    gate_out = jnp.zeros((tm, LANES), F32)
    for kk in range(TOP_K):
        gate_out = jnp.where(lane == kk, es[kk] / den, gate_out)
    idx_ref[...] = idx_out
    gate_ref[...] = gate_out


def _mix(groups, wbr, wbg, wout, nxa, wxq, wxo, nff, wr, br, tm):
    D = D_MODEL
    full = lambda shape: pl.BlockSpec(shape, lambda i: (0,) * len(shape), pipeline_mode=pl.Buffered(1))
    in_specs, args, bounds = [], [], []
    t0 = 0
    for (x, y_rnn, y_gla, proj, kt, kv, S) in groups:
        n = x.shape[0] // tm
        per_b = S // tm
        n_mem = kv.shape[1]
        local = lambda i, t0=t0, n=n: jnp.clip(i - t0, 0, n - 1)
        row = lambda c, local=local: pl.BlockSpec((tm, D), lambda i: (local(i), c))
        in_specs += [
            row(0), row(0), row(0), row(C_MRNN // D), row(C_MGLA // D),
            pl.BlockSpec((None, D, n_mem), lambda i, local=local, per_b=per_b: (local(i) // per_b, 0, 0)),
            pl.BlockSpec((None, n_mem, D), lambda i, local=local, per_b=per_b: (local(i) // per_b, 0, 1)),
        ]
        args += [x, y_rnn, y_gla, proj, proj, kt, kv]
        bounds.append((t0, t0 + n))
        t0 += n
    in_specs += [full((D, D)), full((D, D)), full((D, D)), full((1, D)), full((D, D)), full((D, D)), full((1, D)),
                 full((D, LANES)), full((1, LANES))]
    args += [wbr, wbg, wout, nxa, wxq, wxo, nff, wr, br]
    t_all = t0 * tm
    return pl.pallas_call(
        functools.partial(_mix_kernel, bounds=tuple(bounds)),
        grid=(t0,),
        in_specs=in_specs,
        out_specs=[
            pl.BlockSpec((tm, D), lambda i: (i, 0)),
            pl.BlockSpec((tm, D), lambda i: (i, 0)),
            pl.BlockSpec((tm, LANES), lambda i: (i, 0)),
            pl.BlockSpec((tm, LANES), lambda i: (i, 0)),
        ],
        out_shape=[
            jax.ShapeDtypeStruct((t_all, D), F32),
            jax.ShapeDtypeStruct((t_all, D), BF16),
            jax.ShapeDtypeStruct((t_all, LANES), I32),
            jax.ShapeDtypeStruct((t_all, LANES), F32),
        ],
        compiler_params=_cparams(("parallel",)),
        name="mix",
    )(*args)


def _rank_kernel(idx_ref, loc_ref, tcnt_ref, toff_ref, tcar_ref, tot_ref, carry_ref):
    @pl.when(pl.program_id(0) == 0)
    def _():
        carry_ref[...] = jnp.zeros_like(carry_ref)

    idx = idx_ref[...]
    tm = idx.shape[0]
    lane = lax.broadcasted_iota(I32, (tm, LANES), 1)
    hot = [lane == idx[:, kk:kk + 1] for kk in range(TOP_K)]
    cnt = jnp.zeros((tm, LANES), F32)
    for kk in range(TOP_K):
        cnt = cnt + hot[kk].astype(F32)
    ri = lax.broadcasted_iota(I32, (tm, tm), 0)
    ci = lax.broadcasted_iota(I32, (tm, tm), 1)
    tri = (ci < ri).astype(BF16)
    cum = jnp.dot(tri, cnt.astype(BF16), preferred_element_type=F32)
    tile_cnt = jnp.sum(cnt, axis=0, keepdims=True)
    cnt_al = jnp.floor((tile_cnt + (SEG_ALIGN - 1)) * (1.0 / SEG_ALIGN)) * SEG_ALIGN
    cnt_al8 = jnp.broadcast_to(cnt_al, (SUBLANES, LANES))
    ei = lax.broadcasted_iota(I32, (LANES, LANES), 0)
    ej = lax.broadcasted_iota(I32, (LANES, LANES), 1)
    upper = (ei < ej).astype(BF16)
    off8 = jnp.dot(cnt_al8.astype(BF16), upper, preferred_element_type=F32)
    where_to = cum + off8[0:1, :]
    out = jnp.zeros((tm, LANES), F32)
    for kk in range(TOP_K):
        pk = jnp.sum(jnp.where(hot[kk], where_to, 0.0), axis=-1, keepdims=True)
        out = jnp.where(lane == kk, pk, out)
    loc_ref[...] = out.astype(I32)
    tcnt_ref[...] = cnt_al8.astype(I32)
    toff_ref[...] = off8.astype(I32)
    tcar_ref[...] = carry_ref[...].astype(I32)
    total = carry_ref[...] + cnt_al8
    carry_ref[...] = total
    tot_ref[...] = total.astype(I32)


def _rank(idx, tm):
    T = idx.shape[0]
    nt = T // tm
    per_tile = pl.BlockSpec((None, SUBLANES, LANES), lambda i: (i, 0, 0))
    per_tile_shape = jax.ShapeDtypeStruct((nt, SUBLANES, LANES), I32)
    return pl.pallas_call(
        _rank_kernel,
        grid=(nt,),
        in_specs=[pl.BlockSpec((tm, LANES), lambda i: (i, 0))],
        out_specs=[pl.BlockSpec((tm, LANES), lambda i: (i, 0)), per_tile, per_tile, per_tile,
                   pl.BlockSpec((SUBLANES, LANES), lambda i: (0, 0))],
        out_shape=[jax.ShapeDtypeStruct((T, LANES), I32), per_tile_shape, per_tile_shape, per_tile_shape,
                   jax.ShapeDtypeStruct((SUBLANES, LANES), I32)],
        scratch_shapes=[pltpu.VMEM((SUBLANES, LANES), F32)],
        compiler_params=_cparams(("arbitrary",)),
        name="moe_rank",
    )(idx)


def _segment_copies(n_ref, src_of, dst_of, sem, bits, action):
    for e in range(N_EXPERTS):
        n = n_ref[e]
        for b in bits:
            @pl.when((n & b) != 0)
            def _(e=e, b=b, n=n):
                done = pl.multiple_of(n & ~(2 * b - 1), SEG_ALIGN)
                cp = pltpu.make_async_copy(src_of(e, done, b), dst_of(e, done, b), sem)
                cp.start() if action == "start" else cp.wait()


def _dispatch_kernel(n_ref, off_ref, dst_ref, pst_ref, pln_ref, nu_ref, loc_ref, hn_ref, xblk_ref, xs_ref,
                     zero_ref, sem, zsem, *, n_blocks):
    tm = hn_ref.shape[0]
    loc = loc_ref[...]
    col = lax.broadcasted_iota(I32, (tm, XS_ROWS), 1)
    sel = col == loc[:, 0:1]
    for kk in range(1, TOP_K):
        sel = jnp.logical_or(sel, col == loc[:, kk:kk + 1])
    perm_t = jnp.where(sel, 1.0, 0.0).astype(BF16)
    xs_ref[...] = lax.dot_general(perm_t, hn_ref[...], (((0,), (0,)), ((), ())), preferred_element_type=F32)

    @pl.when(pl.program_id(0) == 0)
    def _():
        zero_ref[...] = jnp.zeros_like(zero_ref)
        for action in ("start", "wait"):
            _segment_copies(
                pln_ref,
                lambda e, done, b: zero_ref.at[pl.ds(0, b), :],
                lambda e, done, b: xblk_ref.at[pl.ds(pl.multiple_of(pst_ref[e], SEG_ALIGN) + done, b), :],
                zsem, PAD_BITS, action)

        def tail_copy(blk):
            return pltpu.make_async_copy(zero_ref, xblk_ref.at[pl.ds(pl.multiple_of(blk * MOE_TM, MOE_TM), MOE_TM), :], zsem)

        def tail_start(blk, c):
            tail_copy(blk).start()
            return c

        def tail_wait(blk, c):
            tail_copy(blk).wait()
            return c

        lax.fori_loop(nu_ref[0], n_blocks, tail_start, 0)
        lax.fori_loop(nu_ref[0], n_blocks, tail_wait, 0)

    for action in ("start", "wait"):
        _segment_copies(
            n_ref,
            lambda e, done, b: xs_ref.at[pl.ds(pl.multiple_of(off_ref[e], SEG_ALIGN) + done, b), :],
            lambda e, done, b: xblk_ref.at[pl.ds(pl.multiple_of(dst_ref[e], SEG_ALIGN) + done, b), :],
            sem, SEG_BITS, action)


def _smem_row():
    return pl.BlockSpec((LANES,), lambda i: (i,), memory_space=pltpu.SMEM)


def _dispatch(seg_n, seg_off, seg_dst, pad_start, pad_len, n_used, loc, hn2, n_blocks, tm):
    T, D = hn2.shape
    whole = pl.BlockSpec((LANES,), lambda i: (0,), memory_space=pltpu.SMEM)
    return pl.pallas_call(
        functools.partial(_dispatch_kernel, n_blocks=n_blocks),
        grid=(T // tm,),
        in_specs=[_smem_row(), _smem_row(), _smem_row(), whole, whole, whole,
                  pl.BlockSpec((tm, LANES), lambda i: (i, 0)),
                  pl.BlockSpec((tm, D), lambda i: (i, 0))],
        out_specs=pl.BlockSpec(memory_space=pl.ANY),
        out_shape=jax.ShapeDtypeStruct((n_blocks * MOE_TM, D), F32),
        scratch_shapes=[pltpu.VMEM((XS_ROWS, D), F32), pltpu.VMEM((MOE_TM, D), F32),
                        pltpu.SemaphoreType.DMA(()), pltpu.SemaphoreType.DMA(())],
        compiler_params=pltpu.CompilerParams(dimension_semantics=("arbitrary",), vmem_limit_bytes=VMEM_LIMIT,
                                             has_side_effects=True),
        name="moe_dispatch",
    )(seg_n, seg_off, seg_dst, pad_start, pad_len, n_used, loc, hn2)


def _expert_kernel(be_ref, nu_ref, x_ref, wg_ref, wu_ref, bg_ref, bu_ref, wd_ref, bd_ref, y_ref):
    del be_ref
    i = pl.program_id(0)

    @pl.when(i < nu_ref[0])
    def _():
        xb = x_ref[...].astype(BF16)
        g = jnp.dot(xb, wg_ref[...], preferred_element_type=F32) + bg_ref[...]
        u = jnp.dot(xb, wu_ref[...], preferred_element_type=F32) + bu_ref[...]
        glu = jnp.minimum(g, SWIGLU_LIMIT)
        lin = jnp.clip(u, -SWIGLU_LIMIT, SWIGLU_LIMIT)
        act = glu * _sigmoid(SWIGLU_ALPHA * glu) * (lin + 1.0)
        y_ref[...] = jnp.dot(act.astype(BF16), wd_ref[...], preferred_element_type=F32) + bd_ref[...]

    @pl.when(i >= nu_ref[0])
    def _():
        y_ref[...] = jnp.zeros_like(y_ref)


def _experts(blk_expert, n_used, x_blk, wg, wu, bg, bu, wd, bd, tm):
    P, D = x_blk.shape
    F = wg.shape[2]
    nb = P // tm
    grid_spec = pltpu.PrefetchScalarGridSpec(
        num_scalar_prefetch=2,
        grid=(nb,),
        in_specs=[
            pl.BlockSpec((tm, D), lambda i, be, nu: (i, 0)),
            pl.BlockSpec((None, D, F), lambda i, be, nu: (be[i], 0, 0)),
            pl.BlockSpec((None, D, F), lambda i, be, nu: (be[i], 0, 0)),
            pl.BlockSpec((None, 1, F), lambda i, be, nu: (be[i], 0, 0)),
            pl.BlockSpec((None, 1, F), lambda i, be, nu: (be[i], 0, 0)),
            pl.BlockSpec((None, F, D), lambda i, be, nu: (be[i], 0, 0)),
            pl.BlockSpec((None, 1, D), lambda i, be, nu: (be[i], 0, 0)),
        ],
        out_specs=pl.BlockSpec((tm, D), lambda i, be, nu: (i, 0)),
    )
    return pl.pallas_call(
        _expert_kernel,
        grid_spec=grid_spec,
        out_shape=jax.ShapeDtypeStruct((P, D), F32),
        compiler_params=_cparams(("arbitrary",)),
        name="moe_experts",
    )(blk_expert, n_used, x_blk, wg, wu, bg, bu, wd, bd)


def _combine_kernel(n_ref, off_ref, dst_ref, loc_ref, gate_ref, x2_ref, nf_ref, y_ref, o_ref, ys_ref, sem):
    tm = x2_ref.shape[0]

    @pl.when(pl.program_id(0) == 0)
    def _():
        ys_ref[...] = jnp.zeros_like(ys_ref)

    for action in ("start", "wait"):
        _segment_copies(
            n_ref,
            lambda e, done, b: y_ref.at[pl.ds(pl.multiple_of(dst_ref[e], SEG_ALIGN) + done, b), :],
            lambda e, done, b: ys_ref.at[pl.ds(pl.multiple_of(off_ref[e], SEG_ALIGN) + done, b), :],
            sem, SEG_BITS, action)

    loc = loc_ref[...]
    gate = gate_ref[...]
    col = lax.broadcasted_iota(I32, (tm, XS_ROWS), 1)
    g = jnp.zeros((tm, XS_ROWS), F32)
    for kk in range(TOP_K):
        g = jnp.where(col == loc[:, kk:kk + 1], gate[:, kk:kk + 1], g)
    moe = jnp.dot(g.astype(BF16), ys_ref[...].astype(BF16), preferred_element_type=F32)
    o_ref[...] = _rms(x2_ref[...] + moe, nf_ref[...])


def _combine(seg_n, seg_off, seg_dst, loc, gate, x2, nf, y_blk, tile0, n_tiles, tm):
    D = x2.shape[1]
    smem = lambda: pl.BlockSpec((LANES,), lambda i: (i + tile0,), memory_space=pltpu.SMEM)
    return pl.pallas_call(
        _combine_kernel,
        grid=(n_tiles,),
        in_specs=[smem(), smem(), smem(),
                  pl.BlockSpec((tm, LANES), lambda i: (i + tile0, 0)),
                  pl.BlockSpec((tm, LANES), lambda i: (i + tile0, 0)),
                  pl.BlockSpec((tm, D), lambda i: (i + tile0, 0)),
                  pl.BlockSpec((1, D), lambda i: (0, 0)),
                  pl.BlockSpec(memory_space=pl.ANY)],
        out_specs=pl.BlockSpec((tm, D), lambda i: (i, 0)),
        out_shape=jax.ShapeDtypeStruct((n_tiles * tm, D), F32),
        scratch_shapes=[pltpu.VMEM((XS_ROWS, D), F32), pltpu.SemaphoreType.DMA(())],
        compiler_params=_cparams(("arbitrary",)),
        name="moe_combine",
    )(seg_n, seg_off, seg_dst, loc, gate, x2, nf, y_blk)


DEINT_GROUP = 2 * LANES
DEINT_COLS = 1024


def _deinterleave_kernel(w_ref, g_ref, u_ref):
    src = lax.broadcasted_iota(I32, (DEINT_GROUP, DEINT_GROUP), 0)
    dst = lax.broadcasted_iota(I32, (DEINT_GROUP, DEINT_GROUP), 1)
    want = jnp.where(dst < LANES, 2 * dst, 2 * (dst - LANES) + 1)
    perm = (src == want).astype(BF16)
    for j in range(DEINT_COLS // DEINT_GROUP):
        w = w_ref[:, j * DEINT_GROUP:(j + 1) * DEINT_GROUP].astype(BF16)
        out = jnp.dot(w, perm, preferred_element_type=F32)
        g_ref[:, j * LANES:(j + 1) * LANES] = out[:, :LANES].astype(BF16)
        u_ref[:, j * LANES:(j + 1) * LANES] = out[:, LANES:].astype(BF16)


def _deinterleave(wgu):
    E, D, F2 = wgu.shape
    half = DEINT_COLS // 2
    out = jax.ShapeDtypeStruct((E, D, F2 // 2), BF16)
    return pl.pallas_call(
        _deinterleave_kernel,
        grid=(E, F2 // DEINT_COLS),
        in_specs=[pl.BlockSpec((None, D, DEINT_COLS), lambda e, j: (e, 0, j))],
        out_specs=[pl.BlockSpec((None, D, half), lambda e, j: (e, 0, j)),
                   pl.BlockSpec((None, D, half), lambda e, j: (e, 0, j))],
        out_shape=[out, out],
        compiler_params=_cparams(("parallel", "parallel")),
        name="deinterleave",
    )(wgu)


def _prep_weights(norm_mix, w_in, conv_w, conv_b, rg_w_a, rg_b_a, rg_w_x, rg_b_x, rg_lambda,
                  gla_w_gk, gla_b_gk, gla_norm, w_branch_rnn, w_branch_gla, w_out,
                  norm_xattn, norm_mem, w_xq, w_xkv, w_xo, norm_ffn, w_router, b_router,
                  w_gate_up, b_gate_up, w_down, b_down, norm_final):
    l = 0
    D = D_MODEL
    w = w_in[l]
    w_main = w[:, :5120]
    w_lr = w[:, 5120:5152]
    w_m = w[:, 5152:7200]
    w_in_r = jnp.concatenate([w_main, w_m, w_lr, jnp.zeros((D, IN_COLS - 7200), F32)], axis=1).astype(BF16)

    p_rnn = jnp.concatenate([conv_w[l], conv_b[l][None], rg_lambda[l], jnp.zeros((1, D), F32)], axis=0)
    p_rnn = p_rnn.reshape(SUBLANES, RNN_BLOCKS, RNN_BW).transpose(1, 0, 2)
    wg_rnn = jnp.concatenate([rg_w_a[l, 0], rg_w_x[l, 0], rg_w_a[l, 1], rg_w_x[l, 1]], axis=2).astype(BF16)
    bg_rnn = jnp.stack([rg_b_a[l, 0], rg_b_x[l, 0], rg_b_a[l, 1], rg_b_x[l, 1]], axis=0)
    bg_rnn = bg_rnn.reshape(4, RNN_BLOCKS, RNN_BW).transpose(1, 0, 2).reshape(RNN_BLOCKS, 1, 4 * RNN_BW)

    wgk = gla_w_gk[l].reshape(2, GLA_RANK, GLA_HEADS, GLA_DK).transpose(0, 2, 1, 3)
    wgk_pad = jnp.zeros((2, GLA_HEADS, LANES, GLA_DK), F32)
    wgk_pad = wgk_pad.at[0, :, 0:GLA_RANK].set(wgk[0]).at[1, :, GLA_RANK:2 * GLA_RANK].set(wgk[1]).astype(BF16)
    bgk = gla_b_gk[l].reshape(2, GLA_HEADS, 1, GLA_DK)
    gn = gla_norm[l][None]

    wr = jnp.concatenate([w_router[l], jnp.zeros((D, LANES - N_EXPERTS), F32)], axis=1)
    br = jnp.concatenate([b_router[l], jnp.zeros((LANES - N_EXPERTS,), F32)])[None]

    wg, wu = _deinterleave(w_gate_up[l])
    return dict(
        norm_mix=norm_mix[l][None], w_in=w_in_r, p_rnn=p_rnn, wg_rnn=wg_rnn, bg_rnn=bg_rnn,
        wgk=wgk_pad, bgk=bgk, gn=gn,
        wbr=w_branch_rnn[l].astype(BF16), wbg=w_branch_gla[l].astype(BF16), wout=w_out[l].astype(BF16),
        nxa=norm_xattn[l][None], nmem=norm_mem[l][None], wxq=w_xq[l].astype(BF16),
        wxkv=w_xkv[l].astype(BF16), wxo=w_xo[l].astype(BF16), nff=norm_ffn[l][None], wr=wr, br=br,
        wg=wg, wu=wu,
        bg=b_gate_up[l][:, None, 0::2], bu=b_gate_up[l][:, None, 1::2],
        wd=w_down[l].astype(BF16), bd=b_down[l][:, None, :],
        nf=norm_final[None],
    )


def _mixer_branches(x, mem, W):
    B, S, D = x.shape
    T = B * S
    n_mem = mem.shape[1]
    xt = x.reshape(T, D)
    proj = _norm_matmul(xt, W["norm_mix"], W["w_in"], min(512, T), IN_TN)
    proj3 = proj.reshape(B, S, IN_COLS)
    y_rnn = _rglru(proj3, W["p_rnn"], W["wg_rnn"], W["bg_rnn"]).reshape(T, D)
    y_gla = _gla(proj3, W["wgk"], W["bgk"], W["gn"]).reshape(T, D)
    kv = _norm_matmul(mem.reshape(B * n_mem, D), W["nmem"], W["wxkv"], min(512, B * n_mem), 2 * D)
    kv = kv.reshape(B, n_mem, 2 * D)
    kt = jnp.swapaxes(kv[:, :, :D], 1, 2)
    return (xt, y_rnn, y_gla, proj, kt, kv, S)


def _moe_and_norm(x2, hn2, idx, gate, W, group_rows):
    T, D = x2.shape
    tt = min(MOE_TT, min(group_rows))
    n_tiles = T // tt
    loc, tcnt, toff, tcar, tot = _rank(idx, tt)
    rows = tot[0, :N_EXPERTS]
    padded = (rows + MOE_TM - 1) // MOE_TM * MOE_TM
    pend = jnp.cumsum(padded)
    pstart = pend - padded
    nb = (T * TOP_K + n_tiles * N_EXPERTS * (SEG_ALIGN - 1)) // MOE_TM + N_EXPERTS
    blk_start = jnp.arange(nb, dtype=I32) * MOE_TM
    blk_expert = jnp.minimum(jnp.sum(pend[None, :] <= blk_start[:, None], axis=1), N_EXPERTS - 1).astype(I32)
    n_used = (pend[-1] // MOE_TM).astype(I32)[None]
    lanes = lambda v: jnp.zeros((LANES,), I32).at[:N_EXPERTS].set(v.astype(I32))
    seg_n = tcnt[:, 0, :].reshape(n_tiles * LANES)
    seg_off = toff[:, 0, :].reshape(n_tiles * LANES)
    seg_dst = (tcar[:, 0, :] + lanes(pstart)[None, :]).reshape(n_tiles * LANES)
    pad_start = lanes(pstart + rows)
    pad_len = lanes(padded - rows)

    x_blk = _dispatch(seg_n, seg_off, seg_dst, pad_start, pad_len, lanes(n_used), loc, hn2, nb, tt)
    y_blk = _experts(blk_expert, n_used, x_blk, W["wg"], W["wu"], W["bg"], W["bu"], W["wd"], W["bd"], MOE_TM)
    outs = []
    tile0 = 0
    for r in group_rows:
        outs.append(_combine(seg_n, seg_off, seg_dst, loc, gate, x2, W["nf"], y_blk, tile0, r // tt, tt))
        tile0 += r // tt
    return outs


def kernel(x_prompt, x_sample, mem_prompt, mem_sample, norm_mix, w_in, conv_w, conv_b, rg_w_a, rg_b_a, rg_w_x, rg_b_x, rg_lambda, gla_w_gk, gla_b_gk, gla_norm, w_branch_rnn, w_branch_gla, w_out, norm_xattn, norm_mem, w_xq, w_xkv, w_xo, norm_ffn, w_router, b_router, w_gate_up, b_gate_up, w_down, b_down, norm_final):
    W = _prep_weights(norm_mix, w_in, conv_w, conv_b, rg_w_a, rg_b_a, rg_w_x, rg_b_x, rg_lambda,
                      gla_w_gk, gla_b_gk, gla_norm, w_branch_rnn, w_branch_gla, w_out,
                      norm_xattn, norm_mem, w_xq, w_xkv, w_xo, norm_ffn, w_router, b_router,
                      w_gate_up, b_gate_up, w_down, b_down, norm_final)
    t_prompt = x_prompt.shape[0] * x_prompt.shape[1]
    t_sample = x_sample.shape[0] * x_sample.shape[1]
    groups = [_mixer_branches(x_prompt, mem_prompt, W), _mixer_branches(x_sample, mem_sample, W)]
    tm_mix = min(512, x_prompt.shape[1], x_sample.shape[1])
    x2, hn2, idx, gate = _mix(groups, W["wbr"], W["wbg"], W["wout"], W["nxa"], W["wxq"], W["wxo"], W["nff"],
                              W["wr"], W["br"], tm_mix)
    y_prompt, y_sample = _moe_and_norm(x2, hn2, idx, gate, W, (t_prompt, t_sample))
    return (y_prompt.reshape(x_prompt.shape), y_sample.reshape(x_sample.shape))
```
